```python
import math, functools
import jax, jax.numpy as jnp
from jax import lax
import numpy as np

D_MODEL = 2048
BATCH = 4
SEQ = 4096
DEPTH = 1

RET_HEADS = 8
RET_DK = 128
RET_DV = 256
RET_CHUNK = 128
ROPE_BASE = 10000.0
NSA_HEADS = 16
NSA_GROUPS = 2
NSA_HPG = NSA_HEADS // NSA_GROUPS
NSA_DH = 128
CMP_BLOCK = 32
CMP_STRIDE = 16
CMP_HIDDEN = 256
SLC_BLOCK = 64
SLC_TOPK = 16
WINDOW = 512
Q_BLOCK = 128
FORCE_SCORE = 1e4
D_FF = 5632
CONV_W = 3
EPS = 1e-6

RET_QW = RET_HEADS * RET_DK
RET_VW = RET_HEADS * RET_DV
NSA_QW = NSA_HEADS * NSA_DH
NSA_KVW = NSA_GROUPS * NSA_DH
SPLIT_SIZES = [RET_QW, RET_QW, RET_VW, RET_VW,
               NSA_QW, 6 * NSA_KVW, 3 * NSA_HEADS,
               2 * D_MODEL]
SPLIT_OFFSETS = [int(o) for o in np.cumsum(SPLIT_SIZES)[:-1]]
IN_COLS = int(sum(SPLIT_SIZES))

kernel_name = "hybrid_retention_nsa_convffn"


def rmsnorm(x, w):
    xf = x.astype(jnp.float32)
    y = xf * lax.rsqrt(jnp.mean(xf * xf, axis=-1, keepdims=True) + EPS)
    return (y * w).astype(x.dtype)


def head_rmsnorm(x):
    xf = x.astype(jnp.float32)
    return (xf * lax.rsqrt(jnp.mean(xf * xf, axis=-1, keepdims=True) + EPS)).astype(x.dtype)


def masked_softmax(s, mask):
    s = jnp.where(mask, s.astype(jnp.float32), -jnp.inf)
    m = jnp.max(s, axis=-1, keepdims=True)
    m = jnp.where(jnp.isfinite(m), m, 0.0)
    p = jnp.exp(s - m)
    return p / jnp.maximum(jnp.sum(p, axis=-1, keepdims=True), 1e-30)


def rotary(x, pos):
    d = x.shape[-1]
    half = d // 2
    freq = ROPE_BASE ** (-jnp.arange(half, dtype=jnp.float32) / half)
    ang = pos.astype(jnp.float32)[:, None] * freq[None, :]
    cos, sin = jnp.cos(ang), jnp.sin(ang)
    x1, x2 = x[..., :half], x[..., half:]
    return jnp.concatenate([x1 * cos - x2 * sin, x1 * sin + x2 * cos], axis=-1).astype(x.dtype)


def retention_chunkwise(q, k, v):
    B, H, S, dk = q.shape
    dv = v.shape[-1]
    C = RET_CHUNK
    nC = S // C
    log_gamma = jnp.log(1.0 - 2.0 ** (-5.0 - jnp.arange(H, dtype=jnp.float32)))
    q = q.reshape(B, H, nC, C, dk)
    k = k.reshape(B, H, nC, C, dk)
    v = v.reshape(B, H, nC, C, dv)
    j = jnp.arange(C, dtype=jnp.float32)
    rel = j[:, None] - j[None, :]
    decay = jnp.where(rel >= 0, jnp.exp(log_gamma[:, None, None] * jnp.maximum(rel, 0.0)), 0.0)
    scores = jnp.einsum('bhcnd,bhcmd->bhcnm', q, k) * decay[None, :, None]
    inner = jnp.einsum('bhcnm,bhcme->bhcne', scores, v)
    k_decay = jnp.exp(log_gamma[:, None] * (C - 1 - j)[None, :])
    kv = jnp.einsum('bhcjd,hj,bhcje->bhcde', k, k_decay, v)
    chunk_decay = jnp.exp(log_gamma * C)[None, :, None, None]

    def step(state, kv_c):
        return state * chunk_decay + kv_c, state

    _, states = lax.scan(step, jnp.zeros((B, H, dk, dv), kv.dtype), jnp.moveaxis(kv, 2, 0))
    states = jnp.moveaxis(states, 0, 2)
    q_decay = jnp.exp(log_gamma[:, None] * (j + 1.0)[None, :])
    cross = jnp.einsum('bhcnd,bhcde->bhcne', q, states) * q_decay[None, :, None, :, None]
    return (inner + cross).reshape(B, H, S, dv)


def compress_blocks(x, tok, pe, w1, w2):
    blocks = x[:, :, tok] + pe
    flat = blocks.reshape(blocks.shape[0], blocks.shape[1], blocks.shape[2], CMP_BLOCK * x.shape[-1])
    return jax.nn.gelu(flat @ w1) @ w2


def native_sparse_attention(q, k_c, v_c, k_s, v_s, k_w, v_w, gates,
                            pe_k, w1_k, w2_k, pe_v, w1_v, w2_v):
    B, G, S, Dh = k_s.shape
    R = NSA_HPG
    n_cmp = (S - CMP_BLOCK) // CMP_STRIDE + 1
    n_slc = S // SLC_BLOCK
    top_n = min(SLC_TOPK, n_slc)
    cmp_start = np.arange(n_cmp) * CMP_STRIDE
    tok = cmp_start[:, None] + np.arange(CMP_BLOCK)[None, :]
    k_cmp = compress_blocks(k_c, tok, pe_k, w1_k, w2_k)
    v_cmp = compress_blocks(v_c, tok, pe_v, w1_v, w2_v)
    cmp_end = jnp.asarray(cmp_start + CMP_BLOCK - 1, jnp.int32)
    slc_start = np.arange(n_slc) * SLC_BLOCK
    overlap = jnp.asarray(((cmp_start[:, None] < slc_start[None, :] + SLC_BLOCK) &
                           (cmp_start[:, None] + CMP_BLOCK > slc_start[None, :])).astype(np.float32))
    k_blocks = k_s.reshape(B, G, n_slc, SLC_BLOCK, Dh)
    v_blocks = v_s.reshape(B, G, n_slc, SLC_BLOCK, Dh)
    pad = ((0, 0), (0, 0), (WINDOW, 0), (0, 0))
    k_wp = jnp.pad(k_w, pad)
    v_wp = jnp.pad(v_w, pad)
    qg = q.reshape(B, G, R, S, Dh)
    gg = gates.reshape(B, G, R, S, 3)
    scale = Dh ** -0.5
    blk_ids = jnp.arange(n_slc, dtype=jnp.int32)
    gather = jax.vmap(jax.vmap(lambda kb, ix: kb[ix]))

    def block(start):
        qb = lax.dynamic_slice_in_dim(qg, start, Q_BLOCK, axis=3) * scale
        t = start + jnp.arange(Q_BLOCK, dtype=jnp.int32)
        s_c = jnp.einsum('bgrqd,bgnd->bgrqn', qb, k_cmp)
        p_c = masked_softmax(s_c, cmp_end[None, :] <= t[:, None])
        o_c = jnp.einsum('bgrqn,bgnd->bgrqd', p_c.astype(v_cmp.dtype), v_cmp)
        imp = jnp.einsum('bgrqn,ns->bgqs', p_c, overlap)
        cur = t // SLC_BLOCK
        forced = (blk_ids[None, :] == 0) | (blk_ids[None, :] == cur[:, None]) | (blk_ids[None, :] == cur[:, None] - 1)
        imp = jnp.where(forced, FORCE_SCORE, imp)
        imp = jnp.where(blk_ids[None, :] > cur[:, None], -FORCE_SCORE, imp)
        _, idx = lax.top_k(imp, top_n)
        kg = gather(k_blocks, idx).reshape(B, G, Q_BLOCK, top_n * SLC_BLOCK, Dh)
        vg = gather(v_blocks, idx).reshape(B, G, Q_BLOCK, top_n * SLC_BLOCK, Dh)
        kpos_s = (idx[..., None] * SLC_BLOCK + jnp.arange(SLC_BLOCK, dtype=jnp.int32)).reshape(B, G, Q_BLOCK, top_n * SLC_BLOCK)
        s_s = jnp.einsum('bgrqd,bgqkd->bgrqk', qb, kg)
        p_s = masked_softmax(s_s, (kpos_s <= t[:, None])[:, :, None])
        o_s = jnp.einsum('bgrqk,bgqkd->bgrqd', p_s.astype(vg.dtype), vg)
        kw = lax.dynamic_slice_in_dim(k_wp, start, WINDOW + Q_BLOCK, axis=2)
        vw = lax.dynamic_slice_in_dim(v_wp, start, WINDOW + Q_BLOCK, axis=2)
        kpos_w = start - WINDOW + jnp.arange(WINDOW + Q_BLOCK, dtype=jnp.int32)
        mask_w = (kpos_w[None, :] <= t[:, None]) & (kpos_w[None, :] > t[:, None] - WINDOW) & (kpos_w[None, :] >= 0)
        s_w = jnp.einsum('bgrqd,bgkd->bgrqk', qb, kw)
        p_w = masked_softmax(s_w, mask_w)
        o_w = jnp.einsum('bgrqk,bgkd->bgrqd', p_w.astype(vw.dtype), vw)
        gb = lax.dynamic_slice_in_dim(gg, start, Q_BLOCK, axis=3)
        return gb[..., 0:1] * o_c + gb[..., 1:2] * o_s + gb[..., 2:3] * o_w

    starts = jnp.arange(S // Q_BLOCK, dtype=jnp.int32) * Q_BLOCK
    out = lax.map(block, starts)
    return out.transpose(1, 0, 4, 2, 3, 5).reshape(B, S, NSA_HEADS * Dh)


def causal_dwconv(u, w, b):
    S = u.shape[1]
    up = jnp.pad(u, ((0, 0), (CONV_W - 1, 0), (0, 0)))
    y = up[:, 0:S] * w[0]
    for tap in range(1, CONV_W):
        y = y + up[:, tap:tap + S] * w[tap]
    return y + b


def setup_inputs(seed: int = 0) -> dict:
    key = jax.random.key(seed)
    ks = jax.random.split(key, 20)
    f32 = jnp.float32

    def nrm(k, shape, fan_in):
        return jax.random.normal(k, shape, f32) * (fan_in ** -0.5)

    L = DEPTH
    return {
        "x": jax.random.normal(ks[0], (BATCH, SEQ, D_MODEL), f32),
        "norm1_w": 1.0 + 0.01 * jax.random.normal(ks[1], (L, D_MODEL), f32),
        "w_in": nrm(ks[2], (L, D_MODEL, IN_COLS), D_MODEL),
        "ret_norm_w": 1.0 + 0.01 * jax.random.normal(ks[3], (L, RET_VW), f32),
        "w_ret_up": nrm(ks[4], (L, RET_VW, D_MODEL), RET_VW),
        "cmp_pe_k": 0.02 * jax.random.normal(ks[5], (L, CMP_BLOCK, NSA_DH), f32),
        "cmp_w1_k": nrm(ks[6], (L, CMP_BLOCK * NSA_DH, CMP_HIDDEN), CMP_BLOCK * NSA_DH),
        "cmp_w2_k": nrm(ks[7], (L, CMP_HIDDEN, NSA_DH), CMP_HIDDEN),
        "cmp_pe_v": 0.02 * jax.random.normal(ks[8], (L, CMP_BLOCK, NSA_DH), f32),
        "cmp_w1_v": nrm(ks[9], (L, CMP_BLOCK * NSA_DH, CMP_HIDDEN), CMP_BLOCK * NSA_DH),
        "cmp_w2_v": nrm(ks[10], (L, CMP_HIDDEN, NSA_DH), CMP_HIDDEN),
        "w_nsa_up": nrm(ks[11], (L, NSA_QW, D_MODEL), NSA_QW),
        "w_out": nrm(ks[12], (L, D_MODEL, D_MODEL), D_MODEL),
        "norm2_w": 1.0 + 0.01 * jax.random.normal(ks[13], (L, D_MODEL), f32),
        "w_ffn_up": nrm(ks[14], (L, D_MODEL, 2 * D_FF), D_MODEL),
        "conv_w": nrm(ks[15], (L, CONV_W, 2 * D_FF), CONV_W),
        "conv_b": 0.01 * jax.random.normal(ks[16], (L, 2 * D_FF), f32),
        "w_ffn_down": nrm(ks[17], (L, D_FF, D_MODEL), D_FF),
        "final_norm_w": 1.0 + 0.01 * jax.random.normal(ks[18], (D_MODEL,), f32),
    }


def reference(x, norm1_w, w_in, ret_norm_w, w_ret_up, cmp_pe_k, cmp_w1_k, cmp_w2_k,
              cmp_pe_v, cmp_w1_v, cmp_w2_v, w_nsa_up, w_out, norm2_w, w_ffn_up,
              conv_w, conv_b, w_ffn_down, final_norm_w):
    B, S, _ = x.shape
    pos = jnp.arange(S, dtype=jnp.int32)

    def heads(t, n, d):
        return t.reshape(B, S, n, d).transpose(0, 2, 1, 3)

    h = x
    for l in range(DEPTH):
        xn = rmsnorm(h, norm1_w[l])
        proj = xn @ w_in[l]
        rq, rk, rv, rg, nq, nkv, ngate, mgate = jnp.split(proj, SPLIT_OFFSETS, axis=-1)
        rq = rotary(heads(rq, RET_HEADS, RET_DK), pos) * (RET_DK ** -0.5)
        rk = rotary(heads(rk, RET_HEADS, RET_DK), pos)
        ret = retention_chunkwise(rq, rk, heads(rv, RET_HEADS, RET_DV))
        ret = head_rmsnorm(ret).transpose(0, 2, 1, 3).reshape(B, S, RET_VW) * ret_norm_w[l]
        y_ret = (jax.nn.silu(rg) * ret) @ w_ret_up[l]
        kc, vc, ksl, vsl, kwn, vwn = [heads(t, NSA_GROUPS, NSA_DH) for t in jnp.split(nkv, 6, axis=-1)]
        br_gates = jax.nn.sigmoid(ngate).reshape(B, S, NSA_HEADS, 3).transpose(0, 2, 1, 3)
        nsa = native_sparse_attention(heads(nq, NSA_HEADS, NSA_DH), kc, vc, ksl, vsl, kwn, vwn, br_gates,
                                      cmp_pe_k[l], cmp_w1_k[l], cmp_w2_k[l],
                                      cmp_pe_v[l], cmp_w1_v[l], cmp_w2_v[l])
        y_nsa = nsa @ w_nsa_up[l]
        g_ret, g_nsa = jnp.split(jax.nn.sigmoid(mgate), 2, axis=-1)
        h = h + (g_ret * y_ret + g_nsa * y_nsa) @ w_out[l]
        xn = rmsnorm(h, norm2_w[l])
        u = causal_dwconv(xn @ w_ffn_up[l], conv_w[l], conv_b[l])
        a, b = jnp.split(u, 2, axis=-1)
        h = h + (jax.nn.silu(a) * b) @ w_ffn_down[l]
    return rmsnorm(h, final_norm_w)
```

```python
import functools
import math

import numpy as np
import jax
import jax.numpy as jnp
from jax import lax
from jax.experimental import pallas as pl
from jax.experimental.pallas import tpu as pltpu

F32 = jnp.float32
BF16 = jnp.bfloat16

D_MODEL = 2048
RET_HEADS = 8
RET_DK = 128
RET_DV = 256
RET_CHUNK = 128
ROPE_BASE = 10000.0
NSA_HEADS = 16
NSA_GROUPS = 2
NSA_HPG = NSA_HEADS // NSA_GROUPS
NSA_DH = 128
CMP_BLOCK = 32
CMP_STRIDE = 16
CMP_HIDDEN = 256
SLC_BLOCK = 64
SLC_TOPK = 16
WINDOW = 512
Q_BLOCK = 128
FORCE_SCORE = 1e4
D_FF = 5632
CONV_W = 3
EPS = 1e-6

RET_QW = RET_HEADS * RET_DK
RET_VW = RET_HEADS * RET_DV
NSA_QW = NSA_HEADS * NSA_DH
NSA_KVW = NSA_GROUPS * NSA_DH

SRC_RQ = 0
SRC_NKV = 2 * RET_QW + 2 * RET_VW + NSA_QW
SRC_GATE = SRC_NKV + 6 * NSA_KVW
SRC_MG = SRC_GATE + 3 * NSA_HEADS
SRC_END = SRC_MG + 2 * D_MODEL

OFF_RQ = 0
OFF_RK = OFF_RQ + RET_QW
OFF_RV = OFF_RK + RET_QW
OFF_RG = OFF_RV + RET_VW
OFF_NQ = OFF_RG + RET_VW
OFF_MG = OFF_NQ + NSA_QW
OFF_KV = OFF_MG + 2 * D_MODEL
OFF_GATE = OFF_KV + 6 * NSA_KVW
LANES = 128
NP = OFF_GATE + NSA_GROUPS * LANES

VMEM_LIMIT = 56 * 1024 * 1024
MASK_BIG = float(2.0 ** 100)
M_INIT = -float(2.0 ** 99)


def _cparams(sem):
    return pltpu.CompilerParams(dimension_semantics=sem, vmem_limit_bytes=VMEM_LIMIT)


def _dot(a, b):
    return jnp.dot(a, b, preferred_element_type=F32)


def _dot_nt(a, b):
    return lax.dot_general(a, b, (((1,), (1,)), ((), ())), preferred_element_type=F32)


def _sigmoid(x):
    return 1.0 / (1.0 + jnp.exp(-x))


def _rms(x, w):
    ms = jnp.mean(x * x, axis=-1, keepdims=True)
    return x * lax.rsqrt(ms + EPS) * w


def _rms_matmul_kernel(x_ref, nw_ref, w_ref, o_ref, xn_ref):
    @pl.when(pl.program_id(1) == 0)
    def _():
        xn_ref[...] = _rms(x_ref[...], nw_ref[...]).astype(BF16)

    o_ref[...] = _dot(xn_ref[...], w_ref[...]).astype(o_ref.dtype)


def _rms_matmul(x, nw, w, *, tm, tn):
    t, d = x.shape
    n = w.shape[1]
    return pl.pallas_call(
        _rms_matmul_kernel,
        grid=(t // tm, n // tn),
        in_specs=[
            pl.BlockSpec((tm, d), lambda i, j: (i, 0)),
            pl.BlockSpec((1, d), lambda i, j: (0, 0)),
            pl.BlockSpec((d, tn), lambda i, j: (0, j)),
        ],
        out_specs=pl.BlockSpec((tm, tn), lambda i, j: (i, j)),
        out_shape=jax.ShapeDtypeStruct((t, n), BF16),
        scratch_shapes=[pltpu.VMEM((tm, d), BF16)],
        compiler_params=_cparams(("parallel", "arbitrary")),
        name="rms_inproj",
    )(x, nw, w)


def _retention_kernel(q_ref, k_ref, v_ref, g_ref, cos_ref, sin_ref, dec_ref, kdec_ref,
                      qdec_ref, cdec_ref, nw_ref, o_ref, state_ref, *, n_chunks):
    c_len = RET_CHUNK
    state_ref[...] = jnp.zeros_like(state_ref)

    def chunk(c, carry):
        r0 = pl.multiple_of(c * c_len, c_len)
        rows = pl.ds(r0, c_len)
        cs = cos_ref[rows, :]
        sn = sin_ref[rows, :]
        q = q_ref[0, rows, :].astype(F32)
        k = k_ref[0, rows, :].astype(F32)
        half = RET_DK // 2
        qr = q * cs + pltpu.roll(q, half, 1) * sn
        kr = k * cs + pltpu.roll(k, half, 1) * sn
        qb = qr.astype(BF16)
        kb = kr.astype(BF16)
        v = v_ref[0, rows, :]
        scores = _dot_nt(qb, kb) * dec_ref[0]
        inner = _dot(scores.astype(BF16), v)
        st = state_ref[...]
        cross = _dot(qb, st.astype(BF16)) * qdec_ref[0]
        kd = (kr * kdec_ref[0]).astype(BF16)
        kv = lax.dot_general(kd, v, (((0,), (0,)), ((), ())), preferred_element_type=F32)
        state_ref[...] = st * cdec_ref[0] + kv
        ret = inner + cross
        ms = jnp.mean(ret * ret, axis=-1, keepdims=True)
        y = ret * lax.rsqrt(ms + EPS) * nw_ref[...]
        g = g_ref[0, rows, :].astype(F32)
        o_ref[0, rows, :] = (y * (g * _sigmoid(g))).astype(o_ref.dtype)
        return carry

    lax.fori_loop(0, n_chunks, chunk, 0)


def _retention_tables(s):
    h = RET_HEADS
    c = RET_CHUNK
    half = RET_DK // 2
    pos = jnp.arange(s, dtype=F32)
    freq = ROPE_BASE ** (-jnp.arange(half, dtype=F32) / half)
    ang = pos[:, None] * freq[None, :]
    cos, sin = jnp.cos(ang), jnp.sin(ang)
    cos_t = jnp.concatenate([cos, cos], axis=-1)
    sin_t = jnp.concatenate([-sin, sin], axis=-1)
    log_gamma = jnp.log(1.0 - 2.0 ** (-5.0 - jnp.arange(h, dtype=F32)))
    j = jnp.arange(c, dtype=F32)
    rel = j[:, None] - j[None, :]
    decay = jnp.where(rel >= 0, jnp.exp(log_gamma[:, None, None] * jnp.maximum(rel, 0.0)), 0.0)
    scale = RET_DK ** -0.5
    k_decay = jnp.exp(log_gamma[:, None] * (c - 1 - j)[None, :])
    q_decay = jnp.exp(log_gamma[:, None] * (j + 1.0)[None, :])
    chunk_decay = jnp.exp(log_gamma * c)
    dec = decay * scale
    kdec = jnp.broadcast_to(k_decay[:, :, None], (h, c, RET_DK))
    qdec = jnp.broadcast_to((q_decay * scale)[:, :, None], (h, c, RET_DV))
    cdec = jnp.broadcast_to(chunk_decay[:, None, None], (h, RET_DK, RET_DV))
    return cos_t, sin_t, dec, kdec, qdec, cdec


def _retention(proj3, ret_norm_w):
    b, s, _ = proj3.shape
    h = RET_HEADS
    cos_t, sin_t, dec, kdec, qdec, cdec = _retention_tables(s)
    kern = functools.partial(_retention_kernel, n_chunks=s // RET_CHUNK)
    return pl.pallas_call(
        kern,
        grid=(b, h),
        in_specs=[
            pl.BlockSpec((1, s, RET_DK), lambda bi, hi: (bi, 0, OFF_RQ // RET_DK + hi)),
            pl.BlockSpec((1, s, RET_DK), lambda bi, hi: (bi, 0, OFF_RK // RET_DK + hi)),
            pl.BlockSpec((1, s, RET_DV), lambda bi, hi: (bi, 0, OFF_RV // RET_DV + hi)),
            pl.BlockSpec((1, s, RET_DV), lambda bi, hi: (bi, 0, OFF_RG // RET_DV + hi)),
            pl.BlockSpec((s, RET_DK), lambda bi, hi: (0, 0)),
            pl.BlockSpec((s, RET_DK), lambda bi, hi: (0, 0)),
            pl.BlockSpec((1, RET_CHUNK, RET_CHUNK), lambda bi, hi: (hi, 0, 0)),
            pl.BlockSpec((1, RET_CHUNK, RET_DK), lambda bi, hi: (hi, 0, 0)),
            pl.BlockSpec((1, RET_CHUNK, RET_DV), lambda bi, hi: (hi, 0, 0)),
            pl.BlockSpec((1, RET_DK, RET_DV), lambda bi, hi: (hi, 0, 0)),
            pl.BlockSpec((1, RET_DV), lambda bi, hi: (0, hi)),
        ],
        out_specs=pl.BlockSpec((1, s, RET_DV), lambda bi, hi: (bi, 0, hi)),
        out_shape=jax.ShapeDtypeStruct((b, s, RET_VW), BF16),
        scratch_shapes=[pltpu.VMEM((RET_DK, RET_DV), F32)],
        compiler_params=_cparams(("parallel", "parallel")),
        name="retention",
    )(proj3, proj3, proj3, proj3, cos_t, sin_t, dec, kdec, qdec, cdec, ret_norm_w)


def _gelu_tanh(x):
    c = math.sqrt(2.0 / math.pi)
    return x * (0.5 * (1.0 + jnp.tanh(c * (x + 0.044715 * (x * x * x)))))


def _compress_kernel(x_ref, pet_ref, peb_ref, w1t_ref, w1b_ref, w2_ref, o_ref, *, n_cmp):
    x = x_ref[0, 0, 0].astype(F32)
    top = _dot((x + pet_ref[0]).astype(BF16), w1t_ref[0])
    bot = _dot((x + peb_ref[0]).astype(BF16), w1b_ref[0])
    n_rows = top.shape[0]
    hid = top + pltpu.roll(bot, n_rows - 1, 0)
    out = _dot(_gelu_tanh(hid).astype(BF16), w2_ref[0])
    row = lax.broadcasted_iota(jnp.int32, out.shape, 0)
    o_ref[0, 0, 0] = jnp.where(row < n_cmp, out, 0.0).astype(o_ref.dtype)


def _compress(xg, pet, peb, w1t, w1b, w2, n_cmp):
    _, b, g, ng, gw = xg.shape
    kern = functools.partial(_compress_kernel, n_cmp=n_cmp)
    return pl.pallas_call(
        kern,
        grid=(2, b, g),
        in_specs=[
            pl.BlockSpec((1, 1, 1, ng, gw), lambda w, bi, gi: (w, bi, gi, 0, 0)),
            pl.BlockSpec((1, 1, gw), lambda w, bi, gi: (w, 0, 0)),
            pl.BlockSpec((1, 1, gw), lambda w, bi, gi: (w, 0, 0)),
            pl.BlockSpec((1, gw, CMP_HIDDEN), lambda w, bi, gi: (w, 0, 0)),
            pl.BlockSpec((1, gw, CMP_HIDDEN), lambda w, bi, gi: (w, 0, 0)),
            pl.BlockSpec((1, CMP_HIDDEN, NSA_DH), lambda w, bi, gi: (w, 0, 0)),
        ],
        out_specs=pl.BlockSpec((1, 1, 1, ng, NSA_DH), lambda w, bi, gi: (w, bi, gi, 0, 0)),
        out_shape=jax.ShapeDtypeStruct((2, b, g, ng, NSA_DH), BF16),
        compiler_params=_cparams(("parallel", "parallel", "parallel")),
        name="nsa_compress",
    )(xg, pet, peb, w1t, w1b, w2)


SLC_TILE = 512
N_SLC_PAD = 128


def _topk_rank(imp_t):
    n_slc = imp_t.shape[0]
    sub = 8
    blocks = [imp_t[v * sub:(v + 1) * sub] for v in range(n_slc // sub)]
    riota = lax.broadcasted_iota(jnp.int32, (sub, imp_t.shape[1]), 0)
    counts = [jnp.zeros((sub, imp_t.shape[1]), jnp.int32) for _ in blocks]
    for sp in range(n_slc):
        vp, rp = divmod(sp, sub)
        row = jnp.broadcast_to(blocks[vp][rp:rp + 1, :], (sub, imp_t.shape[1]))
        for v, blk in enumerate(blocks):
            if v > vp:
                hit = (row >= blk).astype(jnp.int32)
            elif v < vp:
                hit = (row > blk).astype(jnp.int32)
            else:
                hit = jnp.where(riota > rp, (row >= blk).astype(jnp.int32),
                                (row > blk).astype(jnp.int32))
            counts[v] = counts[v] + hit
    return jnp.concatenate(counts, axis=0)


def _nsa_kernel(q_ref, gt_ref, ks_ref, vs_ref, kw_ref, vw_ref, kc_ref, vc_ref, ovt_ref, o_ref,
                kaug_ref, kwp_ref, vwp_ref, qa_ref, m_ref, l_ref, acc_ref, *, seq, n_cmp):
    r_heads = NSA_HPG
    dh = NSA_DH
    qb_len = Q_BLOCK
    rows = r_heads * qb_len
    qi = pl.program_id(2)
    start = pl.multiple_of(qi * qb_len, qb_len)
    n_slc = seq // SLC_BLOCK

    @pl.when(qi == 0)
    def _init():
        kaug_ref[:, 0:dh] = ks_ref[0]
        krow = lax.broadcasted_iota(jnp.int32, (seq, N_SLC_PAD), 0)
        klane = lax.broadcasted_iota(jnp.int32, (seq, N_SLC_PAD), 1)
        kaug_ref[:, dh:dh + N_SLC_PAD] = jnp.where((krow // SLC_BLOCK) == klane, 1.0, 0.0).astype(BF16)
        kwp_ref[0:WINDOW, :] = jnp.zeros((WINDOW, dh), BF16)
        kwp_ref[WINDOW:, :] = kw_ref[0]
        vwp_ref[0:WINDOW, :] = jnp.zeros((WINDOW, dh), BF16)
        vwp_ref[WINDOW:, :] = vw_ref[0]

    scale = dh ** -0.5
    qb = (q_ref[0].astype(F32) * scale).astype(BF16)
    qs = jnp.concatenate([qb[:, r * dh:(r + 1) * dh] for r in range(r_heads)], axis=0)

    def rep(x):
        return jnp.concatenate([x] * r_heads, axis=0)

    n_pad = kc_ref.shape[3]
    tq = start + lax.broadcasted_iota(jnp.int32, (qb_len, n_pad), 0)
    nn = lax.broadcasted_iota(jnp.int32, (qb_len, n_pad), 1)
    valid_c = (nn * CMP_STRIDE + (CMP_BLOCK - 1) <= tq) & (nn < n_cmp)
    s_c = _dot_nt(qs, kc_ref[0, 0, 0]) + rep(jnp.where(valid_c, 0.0, -jnp.inf))
    m_c = jnp.max(s_c, axis=-1, keepdims=True)
    m_c = jnp.where(m_c == -jnp.inf, 0.0, m_c)
    p_c = jnp.exp(s_c - m_c)
    p_c = p_c / jnp.maximum(jnp.sum(p_c, axis=-1, keepdims=True), 1e-30)
    o_c = _dot(p_c.astype(BF16), vc_ref[0, 0, 0])

    p_sum = p_c[0:qb_len]
    for r in range(1, r_heads):
        p_sum = p_sum + p_c[r * qb_len:(r + 1) * qb_len]
    p_hi = p_sum.astype(BF16)
    p_lo = (p_sum - p_hi.astype(F32)).astype(BF16)
    imp_t = _dot_nt(ovt_ref[...], p_hi) + _dot_nt(ovt_ref[...], p_lo)
    sidx = lax.broadcasted_iota(jnp.int32, (n_slc, qb_len), 0)
    cur = (start + lax.broadcasted_iota(jnp.int32, (n_slc, qb_len), 1)) // SLC_BLOCK
    forced = (sidx == 0) | (sidx == cur) | (sidx == cur - 1)
    imp_t = jnp.where(forced, FORCE_SCORE, imp_t)
    imp_t = jnp.where(sidx > cur, -FORCE_SCORE, imp_t)
    rank = _topk_rank(imp_t)
    sel_past = (rank < min(SLC_TOPK, n_slc)) & (sidx < 2 * qi)
    selb = jnp.where(sel_past, 0.0, -MASK_BIG)
    selb = jnp.concatenate([selb, jnp.zeros((N_SLC_PAD - n_slc, qb_len), F32)], axis=0)
    selb_q = selb.T.astype(BF16)
    for r in range(r_heads):
        qa_ref[r * qb_len:(r + 1) * qb_len, 0:dh] = qb[:, r * dh:(r + 1) * dh]
        qa_ref[r * qb_len:(r + 1) * qb_len, dh:dh + N_SLC_PAD] = selb_q

    m_ref[...] = jnp.full(m_ref.shape, M_INIT, F32)
    l_ref[...] = jnp.zeros(l_ref.shape, F32)
    acc_ref[...] = jnp.zeros(acc_ref.shape, F32)

    def flash_update(s, v_tile):
        m_prev = m_ref[...]
        m_new = jnp.maximum(m_prev, jnp.max(s, axis=-1, keepdims=True))
        alpha = jnp.exp(m_prev - m_new)
        p = jnp.exp(s - jnp.concatenate([m_new] * (s.shape[1] // LANES), axis=1))
        l_ref[...] = alpha * l_ref[...] + jnp.sum(p, axis=-1, keepdims=True)
        acc_ref[...] = alpha * acc_ref[...] + _dot(p.astype(BF16), v_tile)
        m_ref[...] = m_new

    def past_tile(j, carry):
        k0 = pl.multiple_of(j * SLC_TILE, SLC_TILE)
        s = _dot_nt(qa_ref[...], kaug_ref[pl.ds(k0, SLC_TILE), :])
        flash_update(s, vs_ref[0, pl.ds(k0, SLC_TILE), :])
        return carry

    n_past = (qi * qb_len + SLC_TILE - 1) // SLC_TILE
    lax.fori_loop(0, n_past, past_tile, 0)

    dq = lax.broadcasted_iota(jnp.int32, (qb_len, qb_len), 0)
    dj = lax.broadcasted_iota(jnp.int32, (qb_len, qb_len), 1)
    causal = rep(jnp.where(dj <= dq, 0.0, -jnp.inf))
    s_d = _dot_nt(qs, ks_ref[0, pl.ds(start, qb_len), :]) + causal
    flash_update(s_d, vs_ref[0, pl.ds(start, qb_len), :])
    o_s = acc_ref[...] / l_ref[...]

    wlen = WINDOW + qb_len
    wq = lax.broadcasted_iota(jnp.int32, (qb_len, wlen), 0)
    wj = lax.broadcasted_iota(jnp.int32, (qb_len, wlen), 1)
    valid_w = (wj > wq) & (wj <= wq + WINDOW) & (wj + start >= WINDOW)
    s_w = _dot_nt(qs, kwp_ref[pl.ds(start, wlen), :]) + rep(jnp.where(valid_w, 0.0, -jnp.inf))
    p_w = jnp.exp(s_w - jnp.max(s_w, axis=-1, keepdims=True))
    l_w = jnp.sum(p_w, axis=-1, keepdims=True)
    o_w = _dot(p_w.astype(BF16), vwp_ref[pl.ds(start, wlen), :]) / l_w

    gt = _sigmoid(gt_ref[0].astype(F32))
    for r in range(r_heads):
        rs = slice(r * qb_len, (r + 1) * qb_len)
        out_r = (gt[:, 3 * r:3 * r + 1] * o_c[rs] + gt[:, 3 * r + 1:3 * r + 2] * o_s[rs]
                 + gt[:, 3 * r + 2:3 * r + 3] * o_w[rs])
        o_ref[0, :, r * dh:(r + 1) * dh] = out_r.astype(o_ref.dtype)


def _nsa_attention(proj3, cmp_kv, ovt, n_cmp):
    b, s, _ = proj3.shape
    g = NSA_GROUPS
    dh = NSA_DH
    gq = NSA_HPG * dh
    rows = NSA_HPG * Q_BLOCK
    kv_blk = OFF_KV // dh

    def kv_spec(which):
        return pl.BlockSpec((1, s, dh), lambda bi, gi, qi: (bi, 0, kv_blk + 2 * which + gi))

    def cmp_spec(which):
        return pl.BlockSpec((1, 1, 1, cmp_kv.shape[3], dh), lambda bi, gi, qi: (which, bi, gi, 0, 0))

    kern = functools.partial(_nsa_kernel, seq=s, n_cmp=n_cmp)
    return pl.pallas_call(
        kern,
        grid=(b, g, s // Q_BLOCK),
        in_specs=[
            pl.BlockSpec((1, Q_BLOCK, gq), lambda bi, gi, qi: (bi, qi, OFF_NQ // gq + gi)),
            pl.BlockSpec((1, Q_BLOCK, LANES), lambda bi, gi, qi: (bi, qi, OFF_GATE // LANES + gi)),
            kv_spec(2), kv_spec(3), kv_spec(4), kv_spec(5),
            cmp_spec(0), cmp_spec(1),
            pl.BlockSpec(ovt.shape, lambda bi, gi, qi: (0, 0)),
        ],
        out_specs=pl.BlockSpec((1, Q_BLOCK, gq), lambda bi, gi, qi: (bi, qi, gi)),
        out_shape=jax.ShapeDtypeStruct((b, s, NSA_QW), BF16),
        scratch_shapes=[
            pltpu.VMEM((s, dh + N_SLC_PAD), BF16),
            pltpu.VMEM((s + WINDOW, dh), BF16),
            pltpu.VMEM((s + WINDOW, dh), BF16),
            pltpu.VMEM((rows, dh + N_SLC_PAD), BF16),
            pltpu.VMEM((rows, LANES), F32),
            pltpu.VMEM((rows, LANES), F32),
            pltpu.VMEM((rows, dh), F32),
        ],
        compiler_params=_cparams(("parallel", "parallel", "arbitrary")),
        name="nsa_attention",
    )(proj3, proj3, proj3, proj3, proj3, proj3, cmp_kv, cmp_kv, ovt)


def _merge_kernel(a_ref, b_ref, wa_ref, wb_ref, ga_ref, gb_ref, o_ref):
    ya = _dot(a_ref[...], wa_ref[...])
    yb = _dot(b_ref[...], wb_ref[...])
    ga = _sigmoid(ga_ref[...].astype(F32))
    gb = _sigmoid(gb_ref[...].astype(F32))
    o_ref[...] = (ga * ya + gb * yb).astype(o_ref.dtype)


def _merge(a, b, wa, wb, proj, *, tm, tn):
    t, d = a.shape
    n = wa.shape[1]
    return pl.pallas_call(
        _merge_kernel,
        grid=(t // tm, n // tn),
        in_specs=[
            pl.BlockSpec((tm, d), lambda i, j: (i, 0)),
            pl.BlockSpec((tm, d), lambda i, j: (i, 0)),
            pl.BlockSpec((d, tn), lambda i, j: (0, j)),
            pl.BlockSpec((d, tn), lambda i, j: (0, j)),
            pl.BlockSpec((tm, tn), lambda i, j: (i, OFF_MG // tn + j)),
            pl.BlockSpec((tm, tn), lambda i, j: (i, (OFF_MG + D_MODEL) // tn + j)),
        ],
        out_specs=pl.BlockSpec((tm, tn), lambda i, j: (i, j)),
        out_shape=jax.ShapeDtypeStruct((t, n), BF16),
        compiler_params=_cparams(("parallel", "parallel")),
        name="mixer_merge",
    )(a, b, wa, wb, proj, proj)


def _matmul_res_kernel(a_ref, w_ref, r_ref, o_ref):
    o_ref[...] = r_ref[...] + _dot(a_ref[...], w_ref[...])


def _matmul_res(a, w, res, *, tm, tn):
    t, d = a.shape
    n = w.shape[1]
    return pl.pallas_call(
        _matmul_res_kernel,
        grid=(t // tm, n // tn),
        in_specs=[
            pl.BlockSpec((tm, d), lambda i, j: (i, 0)),
            pl.BlockSpec((d, tn), lambda i, j: (0, j)),
            pl.BlockSpec((tm, tn), lambda i, j: (i, j)),
        ],
        out_specs=pl.BlockSpec((tm, tn), lambda i, j: (i, j)),
        out_shape=jax.ShapeDtypeStruct((t, n), F32),
        compiler_params=_cparams(("parallel", "parallel")),
        name="out_proj",
    )(a, w, res)


HALO = 16


def _ffn_up_kernel(h_ref, halo_ref, nw_ref, wa_ref, wb_ref, cwa_ref, cwb_ref, cba_ref, cbb_ref,
                   o_ref, xs_ref, *, tiles_per_seq):
    i = pl.program_id(0)

    @pl.when(pl.program_id(1) == 0)
    def _():
        xs_ref[HALO:, :] = _rms(h_ref[...], nw_ref[...]).astype(BF16)
        halo = _rms(halo_ref[...], nw_ref[...])
        seq_start = (i % tiles_per_seq) == 0
        xs_ref[0:HALO, :] = jnp.where(seq_start, 0.0, halo).astype(BF16)

    xs = xs_ref[...]

    def conv_branch(w_ref, cw_ref, cb_ref):
        p = _dot(xs, w_ref[...])
        cw = cw_ref[...]
        u = (p * cw[2:3, :] + pltpu.roll(p, 1, 0) * cw[1:2, :] + pltpu.roll(p, 2, 0) * cw[0:1, :])
        return u[HALO:, :] + cb_ref[...]

    a = conv_branch(wa_ref, cwa_ref, cba_ref)
    b = conv_branch(wb_ref, cwb_ref, cbb_ref)
    o_ref[...] = (a * _sigmoid(a) * b).astype(o_ref.dtype)


def _ffn_up(h, nw, w_up, conv_w, conv_b, seq, *, tm, tn):
    t, d = h.shape
    nb = D_FF // tn
    kern = functools.partial(_ffn_up_kernel, tiles_per_seq=seq // tm)
    hb = tm // HALO
    return pl.pallas_call(
        kern,
        grid=(t // tm, nb),
        in_specs=[
            pl.BlockSpec((tm, d), lambda i, j: (i, 0)),
            pl.BlockSpec((HALO, d), lambda i, j: (jnp.maximum(i * hb - 1, 0), 0)),
            pl.BlockSpec((1, d), lambda i, j: (0, 0)),
            pl.BlockSpec((d, tn), lambda i, j: (0, j)),
            pl.BlockSpec((d, tn), lambda i, j: (0, nb + j)),
            pl.BlockSpec((CONV_W, tn), lambda i, j: (0, j)),
            pl.BlockSpec((CONV_W, tn), lambda i, j: (0, nb + j)),
            pl.BlockSpec((1, tn), lambda i, j: (0, j)),
            pl.BlockSpec((1, tn), lambda i, j: (0, nb + j)),
        ],
        out_specs=pl.BlockSpec((tm, tn), lambda i, j: (i, j)),
        out_shape=jax.ShapeDtypeStruct((t, D_FF), BF16),
        scratch_shapes=[pltpu.VMEM((HALO + tm, d), BF16)],
        compiler_params=_cparams(("parallel", "arbitrary")),
        name="ffn_up_conv",
    )(h, h, nw, w_up, w_up, conv_w, conv_w, conv_b, conv_b)


def _ffn_down_kernel(a_ref, w_ref, r_ref, nw_ref, o_ref, acc_ref, *, final_norm):
    k = pl.program_id(1)

    @pl.when(k == 0)
    def _():
        acc_ref[...] = jnp.zeros_like(acc_ref)

    acc_ref[...] += _dot(a_ref[...], w_ref[...])

    @pl.when(k == pl.num_programs(1) - 1)
    def _():
        h = r_ref[...] + acc_ref[...]
        o_ref[...] = _rms(h, nw_ref[...]) if final_norm else h


def _ffn_down(a, w, res, nw, *, tm, tk, final_norm):
    t, kdim = a.shape
    n = w.shape[1]
    kern = functools.partial(_ffn_down_kernel, final_norm=final_norm)
    return pl.pallas_call(
        kern,
        grid=(t // tm, kdim // tk),
        in_specs=[
            pl.BlockSpec((tm, tk), lambda i, k: (i, k)),
            pl.BlockSpec((tk, n), lambda i, k: (k, 0)),
            pl.BlockSpec((tm, n), lambda i, k: (i, 0)),
            pl.BlockSpec((1, n), lambda i, k: (0, 0)),
        ],
        out_specs=pl.BlockSpec((tm, n), lambda i, k: (i, 0)),
        out_shape=jax.ShapeDtypeStruct((t, n), F32),
        scratch_shapes=[pltpu.VMEM((tm, n), F32)],
        compiler_params=_cparams(("parallel", "arbitrary")),
        name="ffn_down",
    )(a, w, res, nw)


def _permute_w_in(w):
    d = w.shape[0]
    per_group = 3 * NSA_HPG
    pieces = [w[:, SRC_RQ:SRC_NKV], w[:, SRC_MG:SRC_END], w[:, SRC_NKV:SRC_GATE]]
    for gi in range(NSA_GROUPS):
        pieces.append(w[:, SRC_GATE + gi * per_group:SRC_GATE + (gi + 1) * per_group])
        pieces.append(jnp.zeros((d, LANES - per_group), w.dtype))
    return jnp.concatenate(pieces, axis=1).astype(BF16)


def _overlap_t(seq, n_cmp_pad):
    n_cmp = (seq - CMP_BLOCK) // CMP_STRIDE + 1
    n_slc = seq // SLC_BLOCK
    cmp_start = np.arange(n_cmp) * CMP_STRIDE
    slc_start = np.arange(n_slc) * SLC_BLOCK
    ov = ((cmp_start[:, None] < slc_start[None, :] + SLC_BLOCK)
          & (cmp_start[:, None] + CMP_BLOCK > slc_start[None, :])).astype(np.float32)
    ovt = np.zeros((n_slc, n_cmp_pad), np.float32)
    ovt[:, :n_cmp] = ov.T
    return jnp.asarray(ovt, BF16)


def kernel(x, norm1_w, w_in, ret_norm_w, w_ret_up, cmp_pe_k, cmp_w1_k, cmp_w2_k, cmp_pe_v, cmp_w1_v, cmp_w2_v, w_nsa_up, w_out, norm2_w, w_ffn_up, conv_w, conv_b, w_ffn_down, final_norm_w):
    b, s, d = x.shape
    t = b * s
    depth = w_in.shape[0]
    g = NSA_GROUPS
    dh = NSA_DH
    n_cmp = (s - CMP_BLOCK) // CMP_STRIDE + 1
    n_grp = s // CMP_STRIDE
    gw = CMP_STRIDE * dh
    ovt = _overlap_t(s, n_grp)

    h = x.reshape(t, d)
    for l in range(depth):
        proj = _rms_matmul(h, norm1_w[l][None, :], _permute_w_in(w_in[l]), tm=512, tn=1280)
        proj3 = proj.reshape(b, s, NP)

        gated_ret = _retention(proj3, ret_norm_w[l][None, :])

        def groups(which):
            c0 = OFF_KV + which * NSA_KVW
            xc = proj3[:, :, c0:c0 + NSA_KVW].reshape(b, s, g, dh)
            return xc.transpose(0, 2, 1, 3).reshape(b, g, n_grp, gw)

        xg = jnp.stack([groups(0), groups(1)])
        pe = jnp.stack([cmp_pe_k[l], cmp_pe_v[l]])
        half = CMP_BLOCK // 2
        pet = pe[:, :half].reshape(2, 1, gw)
        peb = pe[:, half:].reshape(2, 1, gw)
        w1 = jnp.stack([cmp_w1_k[l], cmp_w1_v[l]]).astype(BF16)
        w2 = jnp.stack([cmp_w2_k[l], cmp_w2_v[l]]).astype(BF16)
        cmp_kv = _compress(xg, pet, peb, w1[:, :gw], w1[:, gw:], w2, n_cmp)

        nsa = _nsa_attention(proj3, cmp_kv, ovt, n_cmp)

        merged = _merge(gated_ret.reshape(t, RET_VW), nsa.reshape(t, NSA_QW),
                        w_ret_up[l].astype(BF16), w_nsa_up[l].astype(BF16), proj, tm=512, tn=512)
        h = _matmul_res(merged, w_out[l].astype(BF16), h, tm=512, tn=512)

        act = _ffn_up(h, norm2_w[l][None, :], w_ffn_up[l].astype(BF16), conv_w[l], conv_b[l][None, :],
                      s, tm=512, tn=512)
        last = l == depth - 1
        h = _ffn_down(act, w_ffn_down[l].astype(BF16), h, final_norm_w[None, :], tm=512, tk=512,
                      final_norm=last)
    return h.reshape(b, s, d)
```
